```python
import jax, jax.numpy as jnp
from jax import lax
import numpy as np

D_MODEL = 2048
BATCH = 8
SEQ = 2048
DEPTH = 4

GRID_W = 64
CTX_LEN = 256
N_MIXERS = 2

ATT_HEADS = 32
ATT_KV_HEADS = 4
ATT_HEAD_DIM = 64
ATT_WINDOW = 128
ATT_BLOCK = 128
ATT_Q_DIM = ATT_HEADS * ATT_HEAD_DIM
ATT_KV_DIM = ATT_KV_HEADS * ATT_HEAD_DIM
ATT_QKV_DIM = ATT_Q_DIM + 2 * ATT_KV_DIM
ROPE_AXIS_DIM = ATT_HEAD_DIM // 2
ROPE_BASE = 10000.0

DN_QK_HEADS = 16
DN_V_HEADS = 32
DN_HEAD_DIM = 128
DN_CONV_W = 5
DN_CHUNK = 64
DN_K_DIM = DN_QK_HEADS * DN_HEAD_DIM
DN_V_DIM = DN_V_HEADS * DN_HEAD_DIM
DN_CONV_DIM = 2 * DN_K_DIM + DN_V_DIM
DN_IN_DIM = DN_CONV_DIM + DN_V_DIM + 4 * DN_V_HEADS

N_EXPERTS = 32
TOP_K = 4
D_EXPERT = 768
SWIGLU_LIMIT = 7.0
SWIGLU_ALPHA = 1.702
MOE_BLOCK = 128

LN_EPS = 1e-5
RMS_EPS = 1e-6
L2_EPS = 1e-6
DEEPNORM_ALPHA = (2.0 * DEPTH) ** 0.25
DEEPNORM_BETA = (8.0 * DEPTH) ** -0.25
N_ATT_LAYERS = (DEPTH + 1) // 2
N_DN_LAYERS = DEPTH // 2

kernel_name = "hybrid_swa_deltanet_moe_dit"


def layer_norm(t, g, b):
    tf = t.astype(jnp.float32)
    mu = jnp.mean(tf, axis=-1, keepdims=True)
    var = jnp.mean(jnp.square(tf - mu), axis=-1, keepdims=True)
    return ((tf - mu) * lax.rsqrt(var + LN_EPS) * g + b).astype(t.dtype)


def axial_rope_tables(rows):
    r, col = jnp.meshgrid(jnp.arange(rows), jnp.arange(GRID_W), indexing="ij")
    pos = jnp.stack([r.reshape(-1), col.reshape(-1)], axis=-1).astype(jnp.float32)
    inv_freq = ROPE_BASE ** (-jnp.arange(0, ROPE_AXIS_DIM, 2, dtype=jnp.float32) / ROPE_AXIS_DIM)
    ang = pos[:, :, None] * inv_freq
    return jnp.cos(ang), jnp.sin(ang)


def apply_axial_rope(t, cos, sin):
    B, n, H, hd = t.shape
    tf = t.astype(jnp.float32).reshape(B, n, H, 2, 2, hd // 4)
    t1, t2 = tf[..., 0, :], tf[..., 1, :]
    cs, sn = cos[None, :, None], sin[None, :, None]
    out = jnp.stack([t1 * cs - t2 * sn, t2 * cs + t1 * sn], axis=-2)
    return out.reshape(B, n, H, hd).astype(t.dtype)


def softmax_with_sink(scores, sink):
    sink_col = jnp.broadcast_to(sink.astype(jnp.float32)[None, :, :, None, None], scores.shape[:-1] + (1,))
    p = jax.nn.softmax(jnp.concatenate([scores, sink_col], axis=-1), axis=-1)
    return p[..., :-1]


def windowed_gqa_sink(u_lat, u_ctx, rope_cos, rope_sin, w_qkv, b_qkv, sink, w_o, b_o, need_ctx_out):
    B, S, _ = u_lat.shape
    G = ATT_HEADS // ATT_KV_HEADS
    scale = ATT_HEAD_DIM ** -0.5
    sink = sink.reshape(ATT_KV_HEADS, G)

    def project(u, rotary):
        n = u.shape[1]
        q, k, v = jnp.split(u @ w_qkv + b_qkv, [ATT_Q_DIM, ATT_Q_DIM + ATT_KV_DIM], axis=-1)
        q = q.reshape(B, n, ATT_HEADS, ATT_HEAD_DIM)
        k = k.reshape(B, n, ATT_KV_HEADS, ATT_HEAD_DIM)
        v = v.reshape(B, n, ATT_KV_HEADS, ATT_HEAD_DIM)
        if rotary:
            q = apply_axial_rope(q, rope_cos, rope_sin)
            k = apply_axial_rope(k, rope_cos, rope_sin)
        return q.reshape(B, n, ATT_KV_HEADS, G, ATT_HEAD_DIM), k, v

    def out_proj(o):
        return o.reshape(o.shape[0], o.shape[1], ATT_Q_DIM) @ w_o + b_o

    q_l, k_l, v_l = project(u_lat, True)
    q_c, k_c, v_c = project(u_ctx, False)

    n_blocks = S // ATT_BLOCK
    span = ATT_BLOCK + 2 * ATT_WINDOW
    pad = ((0, 0), (ATT_WINDOW, ATT_WINDOW), (0, 0), (0, 0))
    k_pad, v_pad = jnp.pad(k_l, pad), jnp.pad(v_l, pad)
    q_blocks = jnp.moveaxis(q_l.reshape(B, n_blocks, ATT_BLOCK, ATT_KV_HEADS, G, ATT_HEAD_DIM), 1, 0)

    def latent_block(args):
        blk, q_b = args
        start = blk * ATT_BLOCK
        k_b = lax.dynamic_slice_in_dim(k_pad, start, span, axis=1)
        v_b = lax.dynamic_slice_in_dim(v_pad, start, span, axis=1)
        q_pos = start + jnp.arange(ATT_BLOCK)
        k_pos = start - ATT_WINDOW + jnp.arange(span)
        valid = (jnp.abs(q_pos[:, None] - k_pos[None, :]) <= ATT_WINDOW) & (k_pos >= 0) & (k_pos < S)
        s_loc = jnp.einsum("bqkgd,bskd->bkgqs", q_b, k_b).astype(jnp.float32) * scale
        s_loc = jnp.where(valid, s_loc, -jnp.inf)
        s_ctx = jnp.einsum("bqkgd,bckd->bkgqc", q_b, k_c).astype(jnp.float32) * scale
        p = softmax_with_sink(jnp.concatenate([s_loc, s_ctx], axis=-1), sink).astype(v_b.dtype)
        return (jnp.einsum("bkgqs,bskd->bqkgd", p[..., :span], v_b)
                + jnp.einsum("bkgqc,bckd->bqkgd", p[..., span:], v_c))

    o_lat = lax.map(latent_block, (jnp.arange(n_blocks), q_blocks))
    o_lat = jnp.moveaxis(o_lat, 0, 1).reshape(B, S, ATT_KV_HEADS, G, ATT_HEAD_DIM)
    y_lat = out_proj(o_lat)
    if not need_ctx_out:
        return y_lat, None
    s_cc = jnp.einsum("bqkgd,bckd->bkgqc", q_c, k_c).astype(jnp.float32) * scale
    p_cc = softmax_with_sink(s_cc, sink).astype(v_c.dtype)
    y_ctx = out_proj(jnp.einsum("bkgqc,bckd->bqkgd", p_cc, v_c))
    return y_lat, y_ctx


def centred_depthwise_conv_silu(t, w):
    half = DN_CONV_W // 2
    y = lax.conv_general_dilated(t, w[:, None, :].astype(t.dtype), window_strides=(1,),
                                 padding=[(half, half)], dimension_numbers=("NWC", "WIO", "NWC"),
                                 feature_group_count=t.shape[-1])
    return jax.nn.silu(y)


def l2_normalise(t):
    tf = t.astype(jnp.float32)
    return tf * lax.rsqrt(jnp.sum(tf * tf, axis=-1, keepdims=True) + L2_EPS)


def chunk_gated_delta(q, k, v, g, beta, state0, with_output):
    B, n, H, dk = q.shape
    dv = v.shape[-1]
    nc = n // DN_CHUNK

    def to_chunks(t):
        t = t.astype(jnp.float32).reshape((B, nc, DN_CHUNK, H, -1))
        return t.transpose(1, 0, 3, 2, 4)

    qc, kc, vc = to_chunks(q), to_chunks(k), to_chunks(v)
    gc = jnp.cumsum(to_chunks(g[..., None])[..., 0], axis=-1)
    bc = to_chunks(beta[..., None])[..., 0]
    idx = jnp.arange(DN_CHUNK)
    incl = idx[:, None] >= idx[None, :]
    decay = jnp.exp(jnp.where(incl, gc[..., :, None] - gc[..., None, :], -jnp.inf))
    m = jnp.where(idx[:, None] > idx[None, :],
                  bc[..., :, None] * jnp.einsum("zbhid,zbhjd->zbhij", kc, kc) * decay, 0.0)
    rhs = jnp.concatenate([vc * bc[..., None], kc * (bc * jnp.exp(gc))[..., None]], axis=-1)
    sol = lax.linalg.triangular_solve(m + jnp.eye(DN_CHUNK, dtype=jnp.float32), rhs,
                                      left_side=True, lower=True, unit_diagonal=True)
    u, w = sol[..., :dv], sol[..., dv:]
    qk = jnp.einsum("zbhid,zbhjd->zbhij", qc, kc) * decay if with_output else None

    def step(S, xs):
        q_i, k_i, u_i, w_i, g_i, qk_i = xs
        v_new = u_i - jnp.einsum("bhcd,bhde->bhce", w_i, S)
        g_last = g_i[..., -1]
        S_next = (S * jnp.exp(g_last)[..., None, None]
                  + jnp.einsum("bhcd,bhce->bhde", k_i * jnp.exp(g_last[..., None] - g_i)[..., None], v_new))
        if not with_output:
            return S_next, None
        o = (jnp.einsum("bhcd,bhde->bhce", q_i * jnp.exp(g_i)[..., None], S)
             + jnp.einsum("bhij,bhje->bhie", qk_i, v_new))
        return S_next, o

    S_final, o = lax.scan(step, state0, (qc, kc, u, w, gc, qk))
    if with_output:
        o = o.transpose(1, 0, 3, 2, 4).reshape(B, n, H, dv)
    return o, S_final


def gated_deltanet(u_lat, u_ctx, w_in, conv_w, a_log, dt_bias, norm_w, w_o, need_ctx_out):
    B = u_lat.shape[0]
    rep = DN_V_HEADS // DN_QK_HEADS

    def project(u):
        n = u.shape[1]
        qkv, z, ab = jnp.split(u @ w_in, [DN_CONV_DIM, DN_CONV_DIM + DN_V_DIM], axis=-1)
        qkv = centred_depthwise_conv_silu(qkv, conv_w)
        q, k, v = jnp.split(qkv, [DN_K_DIM, 2 * DN_K_DIM], axis=-1)
        q = l2_normalise(q.reshape(B, n, DN_QK_HEADS, DN_HEAD_DIM)) * DN_HEAD_DIM ** -0.5
        k = l2_normalise(k.reshape(B, n, DN_QK_HEADS, DN_HEAD_DIM))
        q, k = jnp.repeat(q, rep, axis=2), jnp.repeat(k, rep, axis=2)
        v = v.reshape(B, n, DN_V_HEADS, DN_HEAD_DIM)
        ab = ab.astype(jnp.float32).reshape(B, n, 2, 2, DN_V_HEADS)
        g = -jnp.exp(a_log.astype(jnp.float32)) * jax.nn.softplus(ab[:, :, :, 0] + dt_bias.astype(jnp.float32))
        beta = jax.nn.sigmoid(ab[:, :, :, 1])
        return q, k, v, z, g, beta

    def gated_out(o, z):
        bsz, n = o.shape[:2]
        o = o * lax.rsqrt(jnp.mean(o * o, axis=-1, keepdims=True) + RMS_EPS) * norm_w.astype(jnp.float32)
        o = o * jax.nn.silu(z.astype(jnp.float32).reshape(bsz, n, DN_V_HEADS, DN_HEAD_DIM))
        return o.reshape(bsz, n, DN_V_DIM).astype(z.dtype) @ w_o

    qc, kc, vc, zc, gc, bc = project(u_ctx)
    ql, kl, vl, zl, gl, bl = project(u_lat)
    state0 = jnp.zeros((B, DN_V_HEADS, DN_HEAD_DIM, DN_HEAD_DIM), jnp.float32)
    o_lat, o_ctx = [], []
    for d in range(2):
        flip = (lambda t: jnp.flip(t, axis=1)) if d == 1 else (lambda t: t)
        oc, s_ctx = chunk_gated_delta(flip(qc), flip(kc), flip(vc), flip(gc[:, :, d]), flip(bc[:, :, d]),
                                      state0, need_ctx_out)
        ol, _ = chunk_gated_delta(flip(ql), flip(kl), flip(vl), flip(gl[:, :, d]), flip(bl[:, :, d]),
                                  s_ctx, True)
        o_lat.append(flip(ol))
        if need_ctx_out:
            o_ctx.append(flip(oc))
    y_lat = gated_out(o_lat[0] + o_lat[1], zl)
    if not need_ctx_out:
        return y_lat, None
    return y_lat, gated_out(o_ctx[0] + o_ctx[1], zc)


def moe_ffn(h, w_router, b_router, w_gu, b_gu, w_down, b_down):
    T, D = h.shape
    A = T * TOP_K
    n_blocks = -(-A // MOE_BLOCK) + N_EXPERTS
    logits = (h @ w_router + b_router).astype(jnp.float32)
    top_logit, top_idx = lax.top_k(logits, TOP_K)
    top_w = jax.nn.softmax(top_logit, axis=-1)
    flat_e = top_idx.reshape(A)
    order = jnp.argsort(flat_e)
    e_sorted = flat_e[order]
    tok_sorted = order // TOP_K
    w_sorted = top_w.reshape(A)[order]
    counts = jnp.zeros((N_EXPERTS,), jnp.int32).at[flat_e].add(1)
    padded = (counts + MOE_BLOCK - 1) // MOE_BLOCK * MOE_BLOCK
    pad_end = jnp.cumsum(padded)
    pad_start = pad_end - padded
    start = jnp.cumsum(counts) - counts
    dest = pad_start[e_sorted] + jnp.arange(A, dtype=jnp.int32) - start[e_sorted]
    row_tok = jnp.full((n_blocks * MOE_BLOCK,), T, jnp.int32).at[dest].set(tok_sorted)
    block_e = jnp.minimum(jnp.searchsorted(pad_end, jnp.arange(n_blocks) * MOE_BLOCK, side="right"),
                          N_EXPERTS - 1)
    h_pad = jnp.concatenate([h, jnp.zeros((1, D), h.dtype)], axis=0)
    xb = h_pad[row_tok].reshape(n_blocks, MOE_BLOCK, D)

    def expert_block(args):
        xe, e = args
        gu = xe @ w_gu[e] + b_gu[e]
        gate = jnp.minimum(gu[..., ::2], SWIGLU_LIMIT)
        up = jnp.clip(gu[..., 1::2], -SWIGLU_LIMIT, SWIGLU_LIMIT)
        act = (up + 1.0) * gate * jax.nn.sigmoid(SWIGLU_ALPHA * gate)
        return act @ w_down[e] + b_down[e]

    yb = lax.map(expert_block, (xb, block_e)).reshape(n_blocks * MOE_BLOCK, D)
    return jnp.zeros_like(h).at[tok_sorted].add(w_sorted[:, None].astype(h.dtype) * yb[dest])


def setup_inputs(seed: int = 0) -> dict:
    key = jax.random.key(seed)
    ks = jax.random.split(key, 25)
    D = D_MODEL

    def nrm(k, shape, s):
        return jax.random.normal(k, shape, jnp.float32) * s

    dt = jnp.exp(jax.random.uniform(ks[16], (N_DN_LAYERS, 2, DN_V_HEADS), jnp.float32,
                                    float(np.log(1e-3)), float(np.log(1e-1))))
    return {
        "x": nrm(ks[0], (BATCH, SEQ, D), 1.0),
        "c": nrm(ks[1], (BATCH, D), 1.0),
        "ctx": nrm(ks[2], (BATCH, CTX_LEN, D), 1.0),
        "c_ctx": nrm(ks[3], (D,), 1.0),
        "w_mod": nrm(ks[4], (DEPTH, D, 6 * D), 0.5 * D ** -0.5),
        "b_mod": nrm(ks[5], (DEPTH, 6 * D), 0.02),
        "ln_g": 1.0 + nrm(ks[6], (DEPTH, 2, D), 0.02),
        "ln_b": nrm(ks[7], (DEPTH, 2, D), 0.02),
        "att_w_qkv": nrm(ks[8], (N_ATT_LAYERS, D, ATT_QKV_DIM), D ** -0.5),
        "att_b_qkv": nrm(ks[9], (N_ATT_LAYERS, ATT_QKV_DIM), 0.02),
        "att_sink": nrm(ks[10], (N_ATT_LAYERS, ATT_HEADS), 0.5),
        "att_w_o": nrm(ks[11], (N_ATT_LAYERS, ATT_Q_DIM, D), DEEPNORM_BETA * ATT_Q_DIM ** -0.5),
        "att_b_o": nrm(ks[12], (N_ATT_LAYERS, D), 0.02),
        "dn_w_in": nrm(ks[13], (N_DN_LAYERS, D, DN_IN_DIM), D ** -0.5),
        "dn_conv_w": nrm(ks[14], (N_DN_LAYERS, DN_CONV_W, DN_CONV_DIM), DN_CONV_W ** -0.5),
        "dn_a_log": jnp.log(jax.random.uniform(ks[15], (N_DN_LAYERS, 2, DN_V_HEADS), jnp.float32, 1.0, 16.0)),
        "dn_dt_bias": dt + jnp.log(-jnp.expm1(-dt)),
        "dn_norm_w": 1.0 + nrm(ks[17], (N_DN_LAYERS, DN_HEAD_DIM), 0.02),
        "dn_w_o": nrm(ks[18], (N_DN_LAYERS, DN_V_DIM, D), DEEPNORM_BETA * DN_V_DIM ** -0.5),
        "moe_w_router": nrm(ks[19], (DEPTH, D, N_EXPERTS), D ** -0.5),
        "moe_b_router": nrm(ks[20], (DEPTH, N_EXPERTS), 0.01),
        "moe_w_gu": nrm(ks[21], (DEPTH, N_EXPERTS, D, 2 * D_EXPERT), D ** -0.5),
        "moe_b_gu": nrm(ks[22], (DEPTH, N_EXPERTS, 2 * D_EXPERT), 0.02),
        "moe_w_down": nrm(ks[23], (DEPTH, N_EXPERTS, D_EXPERT, D), DEEPNORM_BETA * D_EXPERT ** -0.5),
        "moe_b_down": nrm(ks[24], (DEPTH, N_EXPERTS, D), 0.02),
    }


def reference(x, c, ctx, c_ctx, w_mod, b_mod, ln_g, ln_b, att_w_qkv, att_b_qkv, att_sink, att_w_o, att_b_o,
              dn_w_in, dn_conv_w, dn_a_log, dn_dt_bias, dn_norm_w, dn_w_o, moe_w_router, moe_b_router,
              moe_w_gu, moe_b_gu, moe_w_down, moe_b_down):
    B, S, D = x.shape
    L = ctx.shape[1]
    rows = S // GRID_W
    rope_cos, rope_sin = axial_rope_tables(rows)
    h_lat, h_ctx = x, ctx
    for i in range(DEPTH):
        last = i == DEPTH - 1
        j = i // N_MIXERS
        m_lat = jnp.split((jax.nn.silu(c) @ w_mod[i] + b_mod[i])[:, None, :], 6, axis=-1)
        m_ctx = jnp.split(jax.nn.silu(c_ctx) @ w_mod[i] + b_mod[i], 6, axis=-1)
        u_lat = h_lat * (1.0 + m_lat[1]) + m_lat[0]
        u_ctx = h_ctx * (1.0 + m_ctx[1]) + m_ctx[0]
        if i % N_MIXERS == 0:
            y_lat, y_ctx = windowed_gqa_sink(u_lat, u_ctx, rope_cos, rope_sin, att_w_qkv[j], att_b_qkv[j],
                                             att_sink[j], att_w_o[j], att_b_o[j], not last)
        else:
            y_lat, y_ctx = gated_deltanet(u_lat, u_ctx, dn_w_in[j], dn_conv_w[j], dn_a_log[j], dn_dt_bias[j],
                                          dn_norm_w[j], dn_w_o[j], not last)
        h_lat = layer_norm(DEEPNORM_ALPHA * h_lat + m_lat[2] * y_lat, ln_g[i, 0], ln_b[i, 0])
        u_lat = h_lat * (1.0 + m_lat[4]) + m_lat[3]
        moe_args = (moe_w_router[i], moe_b_router[i], moe_w_gu[i], moe_b_gu[i], moe_w_down[i], moe_b_down[i])
        if last:
            y_lat = moe_ffn(u_lat.reshape(B * S, D), *moe_args).reshape(B, S, D)
        else:
            h_ctx = layer_norm(DEEPNORM_ALPHA * h_ctx + m_ctx[2] * y_ctx, ln_g[i, 0], ln_b[i, 0])
            u_ctx = h_ctx * (1.0 + m_ctx[4]) + m_ctx[3]
            y = moe_ffn(jnp.concatenate([u_lat.reshape(B * S, D), u_ctx.reshape(B * L, D)], axis=0), *moe_args)
            y_lat, y_ctx = y[:B * S].reshape(B, S, D), y[B * S:].reshape(B, L, D)
            h_ctx = layer_norm(DEEPNORM_ALPHA * h_ctx + m_ctx[5] * y_ctx, ln_g[i, 1], ln_b[i, 1])
        h_lat = layer_norm(DEEPNORM_ALPHA * h_lat + m_lat[5] * y_lat, ln_g[i, 1], ln_b[i, 1])
    return h_lat
```

```python
import functools

import numpy as np
import jax
import jax.numpy as jnp
from jax import lax
from jax.experimental import pallas as pl
from jax.experimental.pallas import tpu as pltpu

F32 = jnp.float32
BF16 = jnp.bfloat16

GRID_W = 64
ATT_HEADS = 32
ATT_KV_HEADS = 4
ATT_HEAD_DIM = 64
ATT_WINDOW = 128
ATT_BLOCK = 128
ROPE_BASE = 10000.0
DN_QK_HEADS = 16
DN_V_HEADS = 32
DN_HEAD_DIM = 128
DN_CONV_W = 5
DN_CHUNK = 64
CHUNK_SHIFT = DN_CHUNK.bit_length() - 1
N_EXPERTS = 32
TOP_K = 4
D_EXPERT = 768
SWIGLU_LIMIT = 7.0
SWIGLU_ALPHA = 1.702
LN_EPS = 1e-5
RMS_EPS = 1e-6
L2_EPS = 1e-6

LANES = 128
SUBLANES = 8
VMEM_LIMIT_CAP = 56 * 1024 * 1024
MOE_TM = 256


def _cparams(n_grid, vmem_bytes):
    return pltpu.CompilerParams(
        dimension_semantics=("arbitrary",) * n_grid,
        vmem_limit_bytes=int(min(max(vmem_bytes, 16 * 1024 * 1024), VMEM_LIMIT_CAP)),
    )


def _split_bf16(a):
    hi = a.astype(BF16)
    lo = (a - hi.astype(F32)).astype(BF16)
    return hi, lo


def _dot(a, b):
    return jnp.dot(a, b, preferred_element_type=F32)


def _dot_nt(a, b):
    return lax.dot_general(a, b, (((1,), (1,)), ((), ())), preferred_element_type=F32)


def _dot3(a, b):
    ah, al = _split_bf16(a)
    bh, bl = _split_bf16(b)
    return _dot(ah, bh) + _dot(ah, bl) + _dot(al, bh)


def _pick(n, cands):
    for c in cands:
        if n % c == 0:
            return c
    return n


def _mm_kernel(*refs, has_bias, passes, silu_in):
    if has_bias:
        x_ref, w_ref, b_ref, o_ref = refs
    else:
        x_ref, w_ref, o_ref = refs
    x = x_ref[...]
    if silu_in:
        x = x * jax.nn.sigmoid(x)
    w = w_ref[...]
    if passes == 1:
        acc = _dot(x.astype(BF16), w.astype(BF16))
    else:
        acc = _dot3(x.astype(F32), w.astype(F32))
    if has_bias:
        acc = acc + b_ref[...]
    o_ref[...] = acc.astype(o_ref.dtype)


def _mm(x, w, b=None, *, out_dtype=F32, passes=1, silu_in=False, m_rows=None, name="mm"):
    M = x.shape[0] if m_rows is None else m_rows
    K, N = w.shape
    tm = _pick(M, (1024, 512, 256, 128, 64, 32, 16, 8))
    tn = _pick(N, (1024, 512, 256, 128) if passes == 1 else (512, 256, 128))
    has_bias = b is not None
    in_specs = [pl.BlockSpec((tm, K), lambda i, j: (i, 0)),
                pl.BlockSpec((K, tn), lambda i, j: (0, j))]
    args = [x, w]
    if has_bias:
        in_specs.append(pl.BlockSpec((1, tn), lambda i, j: (0, j)))
        args.append(b.reshape(1, N).astype(F32))
    tiles = tm * K * x.dtype.itemsize + K * tn * w.dtype.itemsize + tm * tn * 4
    temps = (tm * K + K * tn) * (2 if passes == 1 else 8) + tm * tn * 4
    vmem = 2 * tiles + temps + (4 << 20)
    return pl.pallas_call(
        functools.partial(_mm_kernel, has_bias=has_bias, passes=passes, silu_in=silu_in),
        grid=(M // tm, N // tn),
        in_specs=in_specs,
        out_specs=pl.BlockSpec((tm, tn), lambda i, j: (i, j)),
        out_shape=jax.ShapeDtypeStruct((M, N), out_dtype),
        compiler_params=_cparams(2, vmem),
        name=name,
    )(*args)


def _ln_mod_kernel(*refs, alpha, do_ln, do_mod, do_router):
    it = iter(refs)
    h_ref = next(it)
    if do_ln:
        y_ref, gate_ref, lg_ref, lb_ref = next(it), next(it), next(it), next(it)
    if do_mod:
        sh_ref, sc_ref = next(it), next(it)
    if do_router:
        wr_ref, br_ref = next(it), next(it)
    if do_ln:
        hn_ref = next(it)
    if do_mod:
        u_ref = next(it)
    if do_router:
        lo_ref = next(it)

    h = h_ref[...]
    if do_ln:
        z = alpha * h + gate_ref[0] * y_ref[...]
        mu = jnp.mean(z, axis=-1, keepdims=True)
        zc = z - mu
        var = jnp.mean(zc * zc, axis=-1, keepdims=True)
        h = zc * lax.rsqrt(var + LN_EPS) * lg_ref[...] + lb_ref[...]
        hn_ref[...] = h
    if do_mod:
        u = h * (1.0 + sc_ref[0]) + sh_ref[0]
        u_ref[...] = u.astype(u_ref.dtype)
        if do_router:
            lo_ref[...] = _dot3(u, wr_ref[...]) + br_ref[...]


def _ln_mod(h, n_rows, rows_per_mod, n_lat_rows, *, y=None, gate=None, ln_g=None, ln_b=None,
            shift=None, scale=None, w_router=None, b_router=None, alpha=1.0, name="ln_mod"):
    D = h.shape[1]
    tm = 256
    do_ln, do_mod, do_router = y is not None, shift is not None, w_router is not None
    n_lat_tiles = n_lat_rows // tm
    tiles_per_mod = rows_per_mod // tm

    def mod_idx(i):
        return (jnp.where(i < n_lat_tiles, i // tiles_per_mod, n_lat_rows // rows_per_mod), 0, 0)

    row_spec = pl.BlockSpec((tm, D), lambda i: (i, 0))
    mod_spec = pl.BlockSpec((1, 1, D), mod_idx)
    vec_spec = pl.BlockSpec((1, D), lambda i: (0, 0))
    in_specs, args = [row_spec], [h]
    if do_ln:
        in_specs += [row_spec, mod_spec, vec_spec, vec_spec]
        args += [y, gate, ln_g.reshape(1, D), ln_b.reshape(1, D)]
    if do_mod:
        in_specs += [mod_spec, mod_spec]
        args += [shift, scale]
    if do_router:
        E = w_router.shape[1]
        in_specs += [pl.BlockSpec((D, E), lambda i: (0, 0)), pl.BlockSpec((1, E), lambda i: (0, 0))]
        args += [w_router, b_router.reshape(1, E)]
    out_specs, out_shape = [], []
    if do_ln:
        out_specs.append(row_spec)
        out_shape.append(jax.ShapeDtypeStruct((n_rows, D), F32))
    if do_mod:
        out_specs.append(row_spec)
        out_shape.append(jax.ShapeDtypeStruct((n_rows, D), BF16))
    if do_router:
        out_specs.append(pl.BlockSpec((tm, E), lambda i: (i, 0)))
        out_shape.append(jax.ShapeDtypeStruct((n_rows, E), F32))
    return pl.pallas_call(
        functools.partial(_ln_mod_kernel, alpha=alpha, do_ln=do_ln, do_mod=do_mod, do_router=do_router),
        grid=(n_rows // tm,),
        in_specs=in_specs,
        out_specs=out_specs,
        out_shape=out_shape,
        compiler_params=_cparams(1, 32 << 20),
        name=name,
    )(*args)


def _rope(x, cs, sn):
    lane = lax.broadcasted_iota(jnp.int32, x.shape, 1)
    first = (lane & 31) < 16
    partner = jnp.where(first, pltpu.roll(x, 112, axis=1), pltpu.roll(x, 16, axis=1))
    return x * cs + partner * sn


def _dup_heads(slab):
    lane = lax.broadcasted_iota(jnp.int32, slab.shape, 1)
    lo = lane < 64
    rolled = pltpu.roll(slab, 64, axis=1)
    return jnp.where(lo, slab, rolled), jnp.where(lo, rolled, slab)


def _attn_kernel(*refs, lat, S, L):
    if lat:
        (q_ref, k_ref, v_ref, kc_ref, vc_ref, cs_ref, sn_ref, sink_ref, o_ref,
         kk_s, vv_s, kkc_s, vvc_s) = refs
    else:
        q_ref, kc_ref, vc_ref, sink_ref, o_ref, kkc_s, vvc_s = refs
    blk = pl.program_id(1)
    n_slab = ATT_KV_HEADS * ATT_HEAD_DIM // LANES

    @pl.when(blk == 0)
    def _prep():
        for j in range(n_slab):
            ka, kb = _dup_heads(kc_ref[:, j * LANES:(j + 1) * LANES])
            va, vb = _dup_heads(vc_ref[:, j * LANES:(j + 1) * LANES])
            kkc_s[2 * j] = ka.astype(BF16)
            kkc_s[2 * j + 1] = kb.astype(BF16)
            vvc_s[2 * j] = va.astype(BF16)
            vvc_s[2 * j + 1] = vb.astype(BF16)
            if lat:
                kr = _rope(k_ref[:, j * LANES:(j + 1) * LANES], cs_ref[...], sn_ref[...])
                ka, kb = _dup_heads(kr)
                va, vb = _dup_heads(v_ref[:, j * LANES:(j + 1) * LANES])
                kk_s[2 * j] = ka.astype(BF16)
                kk_s[2 * j + 1] = kb.astype(BF16)
                vv_s[2 * j] = va.astype(BF16)
                vv_s[2 * j + 1] = vb.astype(BF16)

    G = ATT_HEADS // ATT_KV_HEADS
    n_pair = G // 2
    rows = n_pair * ATT_BLOCK
    span = ATT_BLOCK + 2 * ATT_WINDOW
    scale = ATT_HEAD_DIM ** -0.5
    if lat:
        start = pl.multiple_of(jnp.clip(blk * ATT_BLOCK - ATT_WINDOW, 0, S - span), ATT_BLOCK)
        r0 = pl.multiple_of(blk * ATT_BLOCK, ATT_BLOCK)
        cs_b = cs_ref[pl.ds(r0, ATT_BLOCK), :]
        sn_b = sn_ref[pl.ds(r0, ATT_BLOCK), :]
        qi = lax.broadcasted_iota(jnp.int32, (2 * rows, span), 0) & (ATT_BLOCK - 1)
        kj = lax.broadcasted_iota(jnp.int32, (2 * rows, span), 1)
        valid = jnp.abs(qi - kj + (blk * ATT_BLOCK - start)) <= ATT_WINDOW
    lane = lax.broadcasted_iota(jnp.int32, (rows, LANES), 1)
    lo = lane < 64

    for kvh in range(ATT_KV_HEADS):
        slabs = []
        for p in range(n_pair):
            col = (kvh * n_pair + p) * LANES
            qs = q_ref[:, col:col + LANES]
            if lat:
                qs = _rope(qs, cs_b, sn_b)
            slabs.append(qs * scale)
        q4 = jnp.concatenate(slabs, axis=0)
        qst = jnp.concatenate([jnp.where(lo, q4, 0.0), jnp.where(lo, 0.0, q4)], axis=0).astype(BF16)
        sink = sink_ref[kvh]
        s_c = _dot_nt(qst, kkc_s[kvh])
        m = jnp.maximum(jnp.max(s_c, axis=-1, keepdims=True), sink)
        if lat:
            s_w = _dot_nt(qst, kk_s[kvh, pl.ds(start, span), :])
            s_w = jnp.where(valid, s_w, -jnp.inf)
            m = jnp.maximum(m, jnp.max(s_w, axis=-1, keepdims=True))
            p_w = jnp.exp(s_w - m)
        p_c = jnp.exp(s_c - m)
        den = jnp.sum(p_c, axis=-1, keepdims=True) + jnp.exp(sink - m)
        o = _dot(p_c.astype(BF16), vvc_s[kvh])
        if lat:
            den = den + jnp.sum(p_w, axis=-1, keepdims=True)
            o = o + _dot(p_w.astype(BF16), vv_s[kvh, pl.ds(start, span), :])
        o = o / den
        o4 = jnp.where(lo, o[:rows], o[rows:])
        for p in range(n_pair):
            col = (kvh * n_pair + p) * LANES
            o_ref[:, col:col + LANES] = o4[p * ATT_BLOCK:(p + 1) * ATT_BLOCK].astype(o_ref.dtype)


def _attention(qkv, rope_cs, rope_sn, sink_rows, B, S, L, *, lat):
    q_dim = ATT_HEADS * ATT_HEAD_DIM
    kv_dim = ATT_KV_HEADS * ATT_HEAD_DIM
    n_q = S if lat else L
    nb = n_q // ATT_BLOCK
    kcol, vcol = q_dim // kv_dim, q_dim // kv_dim + 1
    lat_rows = B * S
    if lat:
        q_spec = pl.BlockSpec((ATT_BLOCK, q_dim), lambda b, i: (b * nb + i, 0))
    else:
        q_spec = pl.BlockSpec((ATT_BLOCK, q_dim), lambda b, i: (lat_rows // ATT_BLOCK + b * nb + i, 0))
    kc_spec = pl.BlockSpec((L, kv_dim), lambda b, i: (lat_rows // L + b, kcol))
    vc_spec = pl.BlockSpec((L, kv_dim), lambda b, i: (lat_rows // L + b, vcol))
    sink_spec = pl.BlockSpec(sink_rows.shape, lambda b, i: (0, 0, 0))
    scr_c = [pltpu.VMEM((ATT_KV_HEADS, L, LANES), BF16), pltpu.VMEM((ATT_KV_HEADS, L, LANES), BF16)]
    if lat:
        in_specs = [q_spec,
                    pl.BlockSpec((S, kv_dim), lambda b, i: (b, kcol)),
                    pl.BlockSpec((S, kv_dim), lambda b, i: (b, vcol)),
                    kc_spec, vc_spec,
                    pl.BlockSpec((S, LANES), lambda b, i: (0, 0)),
                    pl.BlockSpec((S, LANES), lambda b, i: (0, 0)),
                    sink_spec]
        args = [qkv, qkv, qkv, qkv, qkv, rope_cs, rope_sn, sink_rows]
        scratch = [pltpu.VMEM((ATT_KV_HEADS, S, LANES), BF16), pltpu.VMEM((ATT_KV_HEADS, S, LANES), BF16)] + scr_c
    else:
        in_specs = [q_spec, kc_spec, vc_spec, sink_spec]
        args = [qkv, qkv, qkv, sink_rows]
        scratch = scr_c
    return pl.pallas_call(
        functools.partial(_attn_kernel, lat=lat, S=S, L=L),
        grid=(B, nb),
        in_specs=in_specs,
        out_specs=pl.BlockSpec((ATT_BLOCK, q_dim), lambda b, i: (b * nb + i, 0)),
        out_shape=jax.ShapeDtypeStruct((B * n_q, q_dim), BF16),
        scratch_shapes=scratch,
        compiler_params=_cparams(2, 48 << 20),
        name="attn_lat" if lat else "attn_ctx",
    )(*args)


def _rope_tables(S):
    rows = S // GRID_W
    r, col = np.meshgrid(np.arange(rows), np.arange(GRID_W), indexing="ij")
    pos = jnp.asarray(np.stack([r.reshape(-1), col.reshape(-1)], axis=-1), F32)
    axis_dim = ATT_HEAD_DIM // 2
    inv_freq = ROPE_BASE ** (-jnp.arange(0, axis_dim, 2, dtype=F32) / axis_dim)
    ang = pos[:, :, None] * inv_freq
    cos, sin = jnp.cos(ang), jnp.sin(ang)
    cs64 = jnp.concatenate([cos, cos], axis=-1).reshape(S, ATT_HEAD_DIM)
    sn64 = jnp.concatenate([-sin, sin], axis=-1).reshape(S, ATT_HEAD_DIM)
    return jnp.tile(cs64, (1, 2)), jnp.tile(sn64, (1, 2))


def _sink_rows(sink):
    G = ATT_HEADS // ATT_KV_HEADS
    s = sink.astype(F32).reshape(ATT_KV_HEADS, G // 2, 2)
    s = jnp.transpose(s, (0, 2, 1))
    s = jnp.repeat(s[..., None], ATT_BLOCK, axis=-1)
    return s.reshape(ATT_KV_HEADS, G * ATT_BLOCK, 1)


def _dn_conv_kernel(x_ref, w_ref, o_ref, xp_s, *, n, n_q_tiles, n_qk_tiles):
    half = DN_CONV_W // 2
    j = pl.program_id(1)
    tc = x_ref.shape[1]
    xp_s[0:SUBLANES, :] = jnp.zeros((SUBLANES, tc), F32)
    xp_s[n + SUBLANES:n + 2 * SUBLANES, :] = jnp.zeros((SUBLANES, tc), F32)
    xp_s[SUBLANES:n + SUBLANES, :] = x_ref[...]
    w = w_ref[...]
    R = 256 if n % 256 == 0 else n
    qscale = jnp.where(j < n_q_tiles, DN_HEAD_DIM ** -0.5, 1.0).astype(F32)
    for c in range(n // R):
        base = SUBLANES + c * R - half
        acc = xp_s[base:base + R, :] * w[0:1, :]
        for t in range(1, DN_CONV_W):
            acc = acc + xp_s[base + t:base + t + R, :] * w[t:t + 1, :]
        y = acc * jax.nn.sigmoid(acc)

        @pl.when(j < n_qk_tiles)
        def _norm():
            for hh in range(tc // DN_HEAD_DIM):
                yh = y[:, hh * DN_HEAD_DIM:(hh + 1) * DN_HEAD_DIM]
                ss = jnp.sum(yh * yh, axis=-1, keepdims=True)
                o_ref[c * R:(c + 1) * R, hh * DN_HEAD_DIM:(hh + 1) * DN_HEAD_DIM] = yh * (lax.rsqrt(ss + L2_EPS) * qscale)

        @pl.when(j >= n_qk_tiles)
        def _plain():
            o_ref[c * R:(c + 1) * R, :] = y


def _dn_conv(proj, conv_w, B, n, row_block0, name):
    conv_dim = conv_w.shape[1]
    k_dim = DN_QK_HEADS * DN_HEAD_DIM
    tc = 512
    return pl.pallas_call(
        functools.partial(_dn_conv_kernel, n=n, n_q_tiles=k_dim // tc, n_qk_tiles=2 * k_dim // tc),
        grid=(B, conv_dim // tc),
        in_specs=[pl.BlockSpec((n, tc), lambda b, j: (row_block0 + b, j)),
                  pl.BlockSpec((DN_CONV_W, tc), lambda b, j: (0, j))],
        out_specs=pl.BlockSpec((n, tc), lambda b, j: (b, j)),
        out_shape=jax.ShapeDtypeStruct((B * n, conv_dim), F32),
        scratch_shapes=[pltpu.VMEM((n + 2 * SUBLANES, tc), F32)],
        compiler_params=_cparams(2, 40 << 20),
        name=name,
    )(proj, conv_w)


def _dn_gates_kernel(ab_ref, alog_ref, dtb_ref, o_ref):
    ab = ab_ref[...]
    R = ab.shape[0]
    x = ab + dtb_ref[...]
    softplus = jnp.maximum(x, 0.0) + jnp.log(1.0 + jnp.exp(-jnp.abs(x)))
    g = -jnp.exp(alog_ref[...]) * softplus
    beta = jax.nn.sigmoid(ab)
    ii = lax.broadcasted_iota(jnp.int32, (R, R), 0)
    jj = lax.broadcasted_iota(jnp.int32, (R, R), 1)
    same = (ii >> CHUNK_SHIFT) == (jj >> CHUNK_SHIFT)
    tri_f = jnp.where(same & (jj <= ii), 1.0, 0.0).astype(BF16)
    tri_b = jnp.where(same & (jj >= ii), 1.0, 0.0).astype(BF16)
    g1 = g.astype(BF16)
    r1 = g - g1.astype(F32)
    g2 = r1.astype(BF16)
    g3 = (r1 - g2.astype(F32)).astype(BF16)
    gc_f = _dot(tri_f, g1) + _dot(tri_f, g2) + _dot(tri_f, g3)
    gc_b = _dot(tri_b, g1) + _dot(tri_b, g2) + _dot(tri_b, g3)
    lane = lax.broadcasted_iota(jnp.int32, ab.shape, 1)
    o_ref[...] = jnp.where(lane < DN_V_HEADS, gc_f, jnp.where(lane < 2 * DN_V_HEADS, gc_b, beta))


def _dn_gates(ab, a_log, dt_bias):
    T = ab.shape[0]
    R = 256
    pad = jnp.zeros((2 * DN_V_HEADS,), F32)
    alog = jnp.concatenate([a_log.astype(F32).reshape(-1), pad]).reshape(1, LANES)
    dtb = jnp.concatenate([dt_bias.astype(F32).reshape(-1), pad]).reshape(1, LANES)
    return pl.pallas_call(
        _dn_gates_kernel,
        grid=(T // R,),
        in_specs=[pl.BlockSpec((R, LANES), lambda i: (i, 0)),
                  pl.BlockSpec((1, LANES), lambda i: (0, 0)),
                  pl.BlockSpec((1, LANES), lambda i: (0, 0))],
        out_specs=pl.BlockSpec((R, LANES), lambda i: (i, 0)),
        out_shape=jax.ShapeDtypeStruct((T, LANES), F32),
        compiler_params=_cparams(1, 16 << 20),
        name="dn_gates",
    )(ab, alog, dtb)


NPROB = 4
CATW = NPROB * DN_CHUNK


def _blockdiag4(xcat, bdmask):
    t = jnp.concatenate([xcat] * NPROB, axis=0)
    return jnp.where(bdmask, t, 0.0).astype(BF16)


def _dn_delta_kernel(*refs, n_lat_chunks, n_ctx_chunks, ctx_out):
    C, HD = DN_CHUNK, DN_HEAD_DIM
    (ql_ref, kl_ref, vl_ref, ktl_ref, gcl_ref, grl_ref,
     qc_ref, kc_ref, vc_ref, ktc_ref, gcc_ref, grc_ref) = refs[:12]
    if ctx_out:
        ol_ref, oc_ref = refs[12:14]
        scr = refs[14:]
    else:
        ol_ref, oc_ref = refs[12], None
        scr = refs[13:]
    sol_s, qkd_s, qe_s, kt_s, st_s = scr

    ii = lax.broadcasted_iota(jnp.int32, (C, CATW), 0)
    ll = lax.broadcasted_iota(jnp.int32, (C, CATW), 1)
    jj = ll & (C - 1)
    seg = ll >> CHUNK_SHIFT
    sdiff = jnp.where(seg < 2, ii - jj, jj - ii)
    incl = sdiff >= 0
    strict = sdiff > 0
    eye = jnp.where(ii == jj, 1.0, 0.0).astype(F32)
    bi = lax.broadcasted_iota(jnp.int32, (CATW, CATW), 0) >> CHUNK_SHIFT
    bj = lax.broadcasted_iota(jnp.int32, (CATW, CATW), 1) >> CHUNK_SHIFT
    bdmask = bi == bj

    def cat_cols(cols):
        return jnp.where(seg == 0, cols[0], jnp.where(seg == 1, cols[1], jnp.where(seg == 2, cols[2], cols[3])))

    def phase_a(c, slot, q_ref, k_ref, v_ref, kt_ref, gc_ref, gr_ref):
        r0 = pl.multiple_of(c * C, C)
        q = q_ref[pl.ds(r0, C), :]
        k = k_ref[pl.ds(r0, C), :]
        v2 = v_ref[pl.ds(r0, C), :]
        g8 = gc_ref[0, pl.ds(r0, C), :]
        grow = gr_ref[0, c, 0:1, 0:CATW]
        glast = gr_ref[0, c, 1:2, 0:CATW]
        kt2 = kt_ref[0, c]
        kt4 = jnp.concatenate([kt2, kt2], axis=1)
        gcols = [g8[:, p:p + 1] for p in range(NPROB)]
        bcols = [g8[:, NPROB + p:NPROB + p + 1] for p in range(NPROB)]
        gcat = cat_cols(gcols)
        bcat = cat_cols(bcols)
        decay = jnp.exp(jnp.where(incl, gcat - grow, -jnp.inf))
        kb = k.astype(BF16)
        kt4b = kt4.astype(BF16)
        kk = _dot(kb, kt4b)
        qk = _dot(q.astype(BF16), kt4b)
        m = jnp.where(strict, bcat * kk * decay, 0.0)
        pw = m
        tinv = eye - m
        for _ in range(5):
            pw = _dot(pw.astype(BF16), _blockdiag4(pw, bdmask))
            tinv = tinv + _dot(tinv.astype(BF16), _blockdiag4(pw, bdmask))
        th, tl = _split_bf16(tinv)
        mh, ml = _split_bf16(m)
        thb = _blockdiag4(th.astype(F32), bdmask)
        tlb = _blockdiag4(tl.astype(F32), bdmask)
        mt = _dot(mh, thb) + _dot(mh, tlb) + _dot(ml, thb)
        err = eye - tinv - mt
        tinv = tinv + _dot(th, _blockdiag4(err, bdmask))
        rhs_rows = []
        zero = jnp.zeros((C, 2 * HD), F32)
        for p in range(NPROB):
            hh = p % 2
            eg = jnp.exp(gcols[p])
            rp = jnp.concatenate([v2[:, hh * HD:(hh + 1) * HD] * bcols[p], k * (bcols[p] * eg)], axis=1)
            rhs_rows.append(jnp.concatenate([zero] * p + [rp] + [zero] * (NPROB - 1 - p), axis=1))
            qe_s[slot, :, p * HD:(p + 1) * HD] = q * eg
        rhs_bd = jnp.concatenate(rhs_rows, axis=0).astype(BF16)
        sol_s[slot] = _dot(tinv.astype(BF16), rhs_bd)
        qkd_s[slot] = qk * decay
        kt_s[slot] = kt4 * jnp.exp(glast - grow)

    def phase_b(t, n_chunks, slot0, gr_ref, o_ref):
        cf = t
        cb = n_chunks - 1 - t
        sf = slot0 + cf
        sb = slot0 + cb
        vn = []
        qs_parts = []
        for d, (c, s) in enumerate(((cf, sf), (cb, sb))):
            sol = sol_s[s, :, d * 4 * HD:(d + 1) * 4 * HD]
            u01 = jnp.concatenate([sol[:, 0:HD], sol[:, 2 * HD:3 * HD]], axis=1)
            w01 = jnp.concatenate([sol[:, HD:2 * HD], sol[:, 3 * HD:4 * HD]], axis=1)
            qe01 = qe_s[s, :, d * 2 * HD:(d + 1) * 2 * HD]
            lhs = jnp.concatenate([w01, qe01], axis=0).astype(BF16)
            s0 = st_s[2 * d].astype(BF16)
            s1 = st_s[2 * d + 1].astype(BF16)
            zb = jnp.zeros((HD, HD), BF16)
            sbd = jnp.concatenate([jnp.concatenate([s0, zb], axis=1), jnp.concatenate([zb, s1], axis=1)], axis=0)
            r1 = _dot(lhs, sbd)
            vn.append(u01 - r1[:C])
            qs_parts.append(r1[C:])
        zc = jnp.zeros((C, HD), F32)
        vrows = []
        for p in range(NPROB):
            vp = vn[p // 2][:, (p % 2) * HD:(p % 2 + 1) * HD]
            vrows.append(jnp.concatenate([zc] * p + [vp] + [zc] * (NPROB - 1 - p), axis=1))
        vbd = jnp.concatenate(vrows, axis=0).astype(BF16)
        kt_cat = jnp.concatenate([kt_s[sf, :, 0:2 * C], kt_s[sb, :, 2 * C:4 * C]], axis=1)
        if o_ref is not None:
            qkd_cat = jnp.concatenate([qkd_s[sf, :, 0:2 * C], qkd_s[sb, :, 2 * C:4 * C]], axis=1)
            lhs2 = jnp.concatenate([qkd_cat, kt_cat], axis=0).astype(BF16)
            r2 = _dot(lhs2, vbd)
            o_all = r2[:C]
            ds_all = r2[C:]
            rf = pl.multiple_of(cf * C, C)
            rb = pl.multiple_of(cb * C, C)
            o_ref[pl.ds(rf, C), :] = o_ref[pl.ds(rf, C), :] + o_all[:, 0:2 * HD] + qs_parts[0]
            o_ref[pl.ds(rb, C), :] = o_ref[pl.ds(rb, C), :] + o_all[:, 2 * HD:4 * HD] + qs_parts[1]
        else:
            ds_all = _dot(kt_cat.astype(BF16), vbd)
        egl_f = jnp.exp(gr_ref[0, cf, 2:3, :])
        egl_b = jnp.exp(gr_ref[0, cb, 2:3, :])
        for p in range(NPROB):
            egl = egl_f if p < 2 else egl_b
            st_s[p] = st_s[p] * egl[:, p * HD:(p + 1) * HD] + ds_all[:, p * HD:(p + 1) * HD]

    st_s[...] = jnp.zeros(st_s.shape, F32)
    ol_ref[...] = jnp.zeros(ol_ref.shape, F32)
    if oc_ref is not None:
        oc_ref[...] = jnp.zeros(oc_ref.shape, F32)

    def a_ctx(c, carry):
        phase_a(c, c, qc_ref, kc_ref, vc_ref, ktc_ref, gcc_ref, grc_ref)
        return carry

    def a_lat(c, carry):
        phase_a(c, n_ctx_chunks + c, ql_ref, kl_ref, vl_ref, ktl_ref, gcl_ref, grl_ref)
        return carry

    lax.fori_loop(0, n_ctx_chunks, a_ctx, 0)
    lax.fori_loop(0, n_lat_chunks, a_lat, 0)

    def b_ctx(t, carry):
        phase_b(t, n_ctx_chunks, 0, grc_ref, oc_ref)
        return carry

    def b_lat(t, carry):
        phase_b(t, n_lat_chunks, n_ctx_chunks, grl_ref, ol_ref)
        return carry

    lax.fori_loop(0, n_ctx_chunks, b_ctx, 0)
    lax.fori_loop(0, n_lat_chunks, b_lat, 0)


def _dn_delta(qkv_l, qkv_c, kt_dup, gcols, grows, B, S, L, ctx_out):
    C, HD = DN_CHUNK, DN_HEAD_DIM
    nl, nc = S // C, L // C
    H = DN_QK_HEADS
    kblk = H
    vblk = 2 * H * HD // (2 * HD)
    lat_chunks = B * nl

    def seq_specs(n, nch, row0_blocks, ch0_blocks):
        return [
            pl.BlockSpec((n, HD), lambda b, j: (b, j)),
            pl.BlockSpec((n, HD), lambda b, j: (b, kblk + j)),
            pl.BlockSpec((n, 2 * HD), lambda b, j: (b, vblk + j)),
            pl.BlockSpec((1, nch, HD, 2 * C), lambda b, j: (j, ch0_blocks + b, 0, 0)),
            pl.BlockSpec((1, n, 2 * NPROB), lambda b, j: (j, row0_blocks + b, 0)),
            pl.BlockSpec((1, nch, SUBLANES, NPROB * HD), lambda b, j: (j, ch0_blocks + b, 0, 0)),
        ]

    in_specs = seq_specs(S, nl, 0, 0) + seq_specs(L, nc, B * S // L, lat_chunks // nc)
    args = [qkv_l, qkv_l, qkv_l, kt_dup, gcols, grows, qkv_c, qkv_c, qkv_c, kt_dup, gcols, grows]
    out_specs = [pl.BlockSpec((S, 2 * HD), lambda b, j: (b, j))]
    out_shape = [jax.ShapeDtypeStruct((B * S, DN_V_HEADS * HD), F32)]
    if ctx_out:
        out_specs.append(pl.BlockSpec((L, 2 * HD), lambda b, j: (b, j)))
        out_shape.append(jax.ShapeDtypeStruct((B * L, DN_V_HEADS * HD), F32))
    nslot = nl + nc
    scratch = [pltpu.VMEM((nslot, C, NPROB * 2 * HD), F32),
               pltpu.VMEM((nslot, C, CATW), F32),
               pltpu.VMEM((nslot, C, NPROB * HD), F32),
               pltpu.VMEM((nslot, HD, CATW), F32),
               pltpu.VMEM((NPROB, HD, HD), F32)]
    res = pl.pallas_call(
        functools.partial(_dn_delta_kernel, n_lat_chunks=nl, n_ctx_chunks=nc, ctx_out=ctx_out),
        grid=(B, H),
        in_specs=in_specs,
        out_specs=out_specs,
        out_shape=out_shape,
        scratch_shapes=scratch,
        compiler_params=_cparams(2, 52 << 20),
        name="dn_delta",
    )(*args)
    return res


def _dn_gate_kernel(o_ref, z_ref, nw_ref, y_ref):
    HD = DN_HEAD_DIM
    nw = nw_ref[...]
    for hh in range(o_ref.shape[1] // HD):
        o = o_ref[:, hh * HD:(hh + 1) * HD]
        z = z_ref[:, hh * HD:(hh + 1) * HD]
        ms = jnp.mean(o * o, axis=-1, keepdims=True)
        y = o * lax.rsqrt(ms + RMS_EPS) * nw * (z * jax.nn.sigmoid(z))
        y_ref[:, hh * HD:(hh + 1) * HD] = y.astype(y_ref.dtype)


def _dn_gate(o, proj, z_col0, norm_w, row0, name):
    n, vd = o.shape
    tm, tc = 256, 512
    return pl.pallas_call(
        _dn_gate_kernel,
        grid=(n // tm, vd // tc),
        in_specs=[pl.BlockSpec((tm, tc), lambda i, j: (i, j)),
                  pl.BlockSpec((tm, tc), lambda i, j: (row0 // tm + i, z_col0 // tc + j)),
                  pl.BlockSpec((1, DN_HEAD_DIM), lambda i, j: (0, 0))],
        out_specs=pl.BlockSpec((tm, tc), lambda i, j: (i, j)),
        out_shape=jax.ShapeDtypeStruct((n, vd), BF16),
        compiler_params=_cparams(2, 16 << 20),
        name=name,
    )(o, proj, norm_w.astype(F32).reshape(1, DN_HEAD_DIM))


def _moe_kernel(be_ref, nr_ref, x_ref, wgu_ref, bgu_ref, wd_ref, bd_ref, rw_ref, o_ref):
    i = pl.program_id(0)

    @pl.when(i < nr_ref[0])
    def _compute():
        gu = _dot(x_ref[...], wgu_ref[0]) + bgu_ref[0]
        gate = jnp.minimum(gu[:, :D_EXPERT], SWIGLU_LIMIT)
        up = jnp.clip(gu[:, D_EXPERT:], -SWIGLU_LIMIT, SWIGLU_LIMIT)
        act = (up + 1.0) * gate * jax.nn.sigmoid(SWIGLU_ALPHA * gate)
        y = _dot(act.astype(BF16), wd_ref[0]) + bd_ref[0]
        o_ref[...] = y * rw_ref[...]

    @pl.when(i >= nr_ref[0])
    def _idle():
        o_ref[...] = jnp.zeros(o_ref.shape, o_ref.dtype)


def _moe_experts(xs, block_e, n_real, w_gu, b_gu, w_down, b_down, row_w):
    R, D = xs.shape
    TM = MOE_TM
    NB = R // TM
    E, _, F2 = w_gu.shape
    grid_spec = pltpu.PrefetchScalarGridSpec(
        num_scalar_prefetch=2,
        grid=(NB,),
        in_specs=[
            pl.BlockSpec((TM, D), lambda i, be, nr: (i, 0)),
            pl.BlockSpec((1, D, F2), lambda i, be, nr: (be[i], 0, 0)),
            pl.BlockSpec((1, 1, F2), lambda i, be, nr: (be[i], 0, 0)),
            pl.BlockSpec((1, F2 // 2, D), lambda i, be, nr: (be[i], 0, 0)),
            pl.BlockSpec((1, 1, D), lambda i, be, nr: (be[i], 0, 0)),
            pl.BlockSpec((TM, 1), lambda i, be, nr: (i, 0)),
        ],
        out_specs=pl.BlockSpec((TM, D), lambda i, be, nr: (i, 0)),
    )
    return pl.pallas_call(
        _moe_kernel,
        grid_spec=grid_spec,
        out_shape=jax.ShapeDtypeStruct((R, D), F32),
        compiler_params=_cparams(1, 48 << 20),
        name="moe_experts",
    )(block_e, n_real, xs, w_gu, b_gu.reshape(E, 1, F2), w_down, b_down.reshape(E, 1, D), row_w.reshape(R, 1))


def _moe(u, logits, n_tok, w_gu, b_gu, w_down, b_down):
    T = n_tok
    D = u.shape[1]
    A = T * TOP_K
    TM = MOE_TM
    NB = A // TM + N_EXPERTS
    top_logit, top_idx = lax.top_k(logits, TOP_K)
    top_w = jax.nn.softmax(top_logit, axis=-1)
    flat_e = top_idx.reshape(A).astype(jnp.int32)
    order = jnp.argsort(flat_e, stable=True).astype(jnp.int32)
    e_sorted = flat_e[order]
    tok_sorted = order // TOP_K
    w_sorted = top_w.reshape(A)[order]
    counts = jnp.zeros((N_EXPERTS,), jnp.int32).at[flat_e].add(1)
    padded = (counts + TM - 1) // TM * TM
    pad_end = jnp.cumsum(padded)
    pad_start = pad_end - padded
    start = jnp.cumsum(counts) - counts
    dest = pad_start[e_sorted] + jnp.arange(A, dtype=jnp.int32) - start[e_sorted]
    row_tok = jnp.zeros((NB * TM,), jnp.int32).at[dest].set(tok_sorted)
    row_w = jnp.zeros((NB * TM,), F32).at[dest].set(w_sorted)
    block_e = jnp.minimum(jnp.searchsorted(pad_end, jnp.arange(NB, dtype=jnp.int32) * TM, side="right"),
                          N_EXPERTS - 1).astype(jnp.int32)
    n_real = (pad_end[-1] // TM).astype(jnp.int32).reshape(1)
    pos = jnp.zeros((A,), jnp.int32).at[order].set(dest)
    xs = u[row_tok]
    wgu = jnp.concatenate([w_gu[..., 0::2], w_gu[..., 1::2]], axis=-1).astype(BF16)
    bgu = jnp.concatenate([b_gu[..., 0::2], b_gu[..., 1::2]], axis=-1).astype(F32)
    yb = _moe_experts(xs, block_e, n_real, wgu, bgu, w_down.astype(BF16), b_down.astype(F32), row_w)
    return yb[pos].reshape(T, TOP_K, D).sum(axis=1)


def _att_mixer(u, B, S, L, w_qkv, b_qkv, sink, w_o, b_o, rope_cs, rope_sn, ctx_out):
    qkv = _mm(u, w_qkv.astype(BF16), b_qkv, name="att_qkv")
    sink_rows = _sink_rows(sink)
    o = _attention(qkv, rope_cs, rope_sn, sink_rows, B, S, L, lat=True)
    if ctx_out:
        o_c = _attention(qkv, rope_cs, rope_sn, sink_rows, B, S, L, lat=False)
        o = jnp.concatenate([o, o_c], axis=0)
    return _mm(o, w_o.astype(BF16), b_o, name="att_out")


def _dn_mixer(u, B, S, L, w_in, conv_w, a_log, dt_bias, norm_w, w_o, ctx_out):
    C, HD, H = DN_CHUNK, DN_HEAD_DIM, DN_QK_HEADS
    NL, T = B * S, B * (S + L)
    conv_dim = conv_w.shape[1]
    n_main = conv_dim + DN_V_HEADS * HD
    ab_perm = np.array([d * 2 * DN_V_HEADS + ab * DN_V_HEADS + hd
                        for ab in range(2) for d in range(2) for hd in range(DN_V_HEADS)])
    proj = _mm(u, w_in[:, :n_main].astype(BF16), name="dn_in")
    ab = _mm(u, w_in[:, n_main:][:, ab_perm].astype(BF16), name="dn_ab")
    qkv_l = _dn_conv(proj, conv_w, B, S, 0, "dn_conv_lat")
    qkv_c = _dn_conv(proj, conv_w, B, L, NL // L, "dn_conv_ctx")
    gb = _dn_gates(ab, a_log, dt_bias)
    nch = T // C
    gc = gb[:, :2 * DN_V_HEADS].reshape(T, 2, H, 2)
    bt = gb[:, 2 * DN_V_HEADS:].reshape(T, 2, H, 2)
    gc = jnp.transpose(gc, (2, 0, 1, 3)).reshape(H, T, NPROB)
    bt = jnp.transpose(bt, (2, 0, 1, 3)).reshape(H, T, NPROB)
    gcols = jnp.concatenate([gc, bt], axis=-1)
    gch = gc.reshape(H, nch, C, NPROB)
    grow = jnp.transpose(gch, (0, 1, 3, 2)).reshape(H, nch, 1, CATW)
    gend = jnp.concatenate([gch[:, :, C - 1, :2], gch[:, :, 0, 2:]], axis=-1)
    gend64 = jnp.repeat(gend, C, axis=-1).reshape(H, nch, 1, CATW)
    gend128 = jnp.repeat(gend, HD, axis=-1).reshape(H, nch, 1, NPROB * HD)
    padw = ((0, 0), (0, 0), (0, 0), (0, NPROB * HD - CATW))
    grows = jnp.concatenate([jnp.pad(grow, padw), jnp.pad(gend64, padw), gend128,
                             jnp.zeros((H, nch, SUBLANES - 3, NPROB * HD), F32)], axis=2)
    k_all = jnp.concatenate([qkv_l[:, H * HD:2 * H * HD], qkv_c[:, H * HD:2 * H * HD]], axis=0)
    kt = jnp.transpose(k_all.reshape(nch, C, H, HD), (2, 0, 3, 1))
    kt_dup = jnp.concatenate([kt, kt], axis=-1)
    res = _dn_delta(qkv_l, qkv_c, kt_dup, gcols, grows, B, S, L, ctx_out)
    g = _dn_gate(res[0], proj, conv_dim, norm_w, 0, "dn_gate_lat")
    if ctx_out:
        g_c = _dn_gate(res[1], proj, conv_dim, norm_w, NL, "dn_gate_ctx")
        g = jnp.concatenate([g, g_c], axis=0)
    return _mm(g, w_o.astype(BF16), name="dn_out")


def kernel(x, c, ctx, c_ctx, w_mod, b_mod, ln_g, ln_b, att_w_qkv, att_b_qkv, att_sink, att_w_o, att_b_o,
           dn_w_in, dn_conv_w, dn_a_log, dn_dt_bias, dn_norm_w, dn_w_o, moe_w_router, moe_b_router,
           moe_w_gu, moe_b_gu, moe_w_down, moe_b_down):
    B, S, D = x.shape
    L = ctx.shape[1]
    depth = w_mod.shape[0]
    NL, NC = B * S, B * L
    T = NL + NC
    alpha = (2.0 * depth) ** 0.25

    n_mod = B + 1
    cpad = jnp.concatenate([c, c_ctx[None, :], jnp.zeros((2 * SUBLANES - n_mod, D), F32)], axis=0)
    mods = []
    for i in range(depth):
        m = _mm(cpad, w_mod[i], b_mod[i], passes=3, silu_in=True, name="adaln")
        mods.append([m[:n_mod, k * D:(k + 1) * D].reshape(n_mod, 1, D) for k in range(6)])

    rope_cs, rope_sn = _rope_tables(S)
    h = jnp.concatenate([x.reshape(NL, D), ctx.reshape(NC, D)], axis=0)
    (u,) = _ln_mod(h, T, S, NL, shift=mods[0][0], scale=mods[0][1], name="modulate")

    for i in range(depth):
        last = i == depth - 1
        j = i // 2
        n_out = NL if last else T
        if i % 2 == 0:
            y = _att_mixer(u, B, S, L, att_w_qkv[j], att_b_qkv[j], att_sink[j], att_w_o[j], att_b_o[j],
                           rope_cs, rope_sn, not last)
        else:
            y = _dn_mixer(u, B, S, L, dn_w_in[j], dn_conv_w[j], dn_a_log[j], dn_dt_bias[j], dn_norm_w[j],
                          dn_w_o[j], not last)
        h, u2, logits = _ln_mod(h, n_out, S, NL, y=y, gate=mods[i][2], ln_g=ln_g[i, 0], ln_b=ln_b[i, 0],
                                shift=mods[i][3], scale=mods[i][4], w_router=moe_w_router[i],
                                b_router=moe_b_router[i], alpha=alpha, name="ln_mod_router")
        y2 = _moe(u2, logits, n_out, moe_w_gu[i], moe_b_gu[i], moe_w_down[i], moe_b_down[i])
        if last:
            (h,) = _ln_mod(h, n_out, S, NL, y=y2, gate=mods[i][5], ln_g=ln_g[i, 1], ln_b=ln_b[i, 1],
                           alpha=alpha, name="ln_final")
        else:
            h, u = _ln_mod(h, n_out, S, NL, y=y2, gate=mods[i][5], ln_g=ln_g[i, 1], ln_b=ln_b[i, 1],
                           shift=mods[i + 1][0], scale=mods[i + 1][1], alpha=alpha, name="ln_mod")
    return h.reshape(B, S, D)
```
